```python
import jax, jax.numpy as jnp
from jax import lax
import numpy as np

D_MODEL = 1024
BATCH = 8
SEQ = 8192
DEPTH = 2
DEC_BATCH = 8
DEC_SEQ = 32
PAST_LEN = 4096

CHUNK = 64
N_RET_HEADS = 4
RET_HEAD_DIM = 128
RET_DIM = N_RET_HEADS * RET_HEAD_DIM
CONV_DIM = D_MODEL - RET_DIM
CONV_WIDTH = 3
MIX_DIM = RET_DIM + CONV_DIM
IN_DIM = 4 * RET_DIM + 3 * CONV_DIM
D_FF = 2816
N_MOD = 9
ROPE_BASE = 10000.0
EPS = 1e-6

kernel_name = "hybrid_retention_shortconv_macaron_stream"


def rmsnorm(x, g):
    x32 = x.astype(jnp.float32)
    y = x32 * lax.rsqrt(jnp.mean(x32 * x32, axis=-1, keepdims=True) + EPS)
    return (y * g.astype(jnp.float32)).astype(x.dtype)


def modulate(h, shift, scale):
    return h * (1 + scale[:, None, :]) + shift[:, None, :]


def swiglu(h, w_in, w_out):
    gate, up = jnp.split(h @ w_in, 2, axis=-1)
    return (jax.nn.silu(gate) * up) @ w_out


def rotary(x, pos):
    half = x.shape[-1] // 2
    inv = ROPE_BASE ** (-jnp.arange(half, dtype=jnp.float32) / half)
    ang = pos.astype(jnp.float32)[:, None] * inv[None, :]
    cos = jnp.cos(ang)[None, :, None, :]
    sin = jnp.sin(ang)[None, :, None, :]
    x32 = x.astype(jnp.float32)
    x1, x2 = x32[..., :half], x32[..., half:]
    return jnp.concatenate([x1 * cos - x2 * sin, x1 * sin + x2 * cos], axis=-1).astype(x.dtype)


def retention_log_gamma():
    return jnp.log1p(-jnp.exp2(-5.0 - jnp.arange(N_RET_HEADS, dtype=jnp.float32)))


def retention_chunk(q, k, v, S, log_gamma):
    L = q.shape[1]
    idx = jnp.arange(L, dtype=jnp.float32)
    dist = jnp.abs(idx[:, None] - idx[None, :])
    decay = jnp.exp(log_gamma[:, None, None] * dist)
    scores = jnp.einsum('bihd,bjhd->bhij', q, k) * decay[None]
    intra = jnp.einsum('bhij,bjhe->bihe', scores, v)
    cross_decay = jnp.exp(log_gamma[:, None] * (idx[None, :] + 1.0))
    cross = jnp.einsum('bihd,bhde->bihe', q, S) * jnp.transpose(cross_decay)[None, :, :, None]
    kdec = jnp.exp(log_gamma[:, None] * (L - 1.0 - idx[None, :]))
    S_new = jnp.exp(log_gamma * L)[None, :, None, None] * S + jnp.einsum('bjhd,hj,bjhe->bhde', k, kdec, v)
    return intra + cross, S_new


def retention_prompt(q, k, v, log_gamma):
    B, L, H, Dk = q.shape
    Dv = v.shape[-1]
    nc = L // CHUNK

    def to_chunks(t):
        return jnp.moveaxis(t.reshape(B, nc, CHUNK, H, t.shape[-1]), 1, 0)

    def step(S, qkv):
        qc, kc, vc = qkv
        o, S = retention_chunk(qc, kc, vc, S, log_gamma)
        return S, o

    S0 = jnp.zeros((B, H, Dk, Dv), jnp.float32)
    S_fin, o = lax.scan(step, S0, (to_chunks(q), to_chunks(k), to_chunks(v)))
    o = jnp.moveaxis(o, 0, 1).reshape(B, L, H, Dv)
    return o, S_fin


def short_conv(u, hist, w, b):
    L = u.shape[1]
    ext = jnp.concatenate([hist.astype(u.dtype), u], axis=1)
    y = b + sum(ext[:, j:j + L] * w[j] for j in range(CONV_WIDTH))
    return y, ext[:, -(CONV_WIDTH - 1):]


def mixer(h, pos, S, hist, chunked, w_in, w_out, gn_g, conv_w, conv_b):
    B, L, _ = h.shape
    p = h @ w_in
    q, k, v, g, bg, cg, hin = jnp.split(
        p, [RET_DIM, 2 * RET_DIM, 3 * RET_DIM, 4 * RET_DIM, 4 * RET_DIM + CONV_DIM, 4 * RET_DIM + 2 * CONV_DIM], axis=-1)
    q = rotary(q.reshape(B, L, N_RET_HEADS, RET_HEAD_DIM), pos)
    k = rotary(k.reshape(B, L, N_RET_HEADS, RET_HEAD_DIM), pos) * (RET_HEAD_DIM ** -0.5)
    v = v.reshape(B, L, N_RET_HEADS, RET_HEAD_DIM)
    log_gamma = retention_log_gamma()
    if chunked:
        o, S_new = retention_prompt(q, k, v, log_gamma)
    else:
        o, S_new = retention_chunk(q, k, v, S, log_gamma)
    o32 = o.astype(jnp.float32)
    mu = jnp.mean(o32, axis=-1, keepdims=True)
    var = jnp.mean(jnp.square(o32 - mu), axis=-1, keepdims=True)
    o_n = ((o32 - mu) * lax.rsqrt(var + EPS)).reshape(B, L, RET_DIM) * gn_g.astype(jnp.float32)
    ret_out = (jax.nn.silu(g.astype(jnp.float32)) * o_n).astype(h.dtype)
    y, hist_new = short_conv(cg * hin, hist, conv_w, conv_b)
    conv_out = (bg * y).astype(h.dtype)
    out = jnp.concatenate([ret_out, conv_out], axis=-1) @ w_out
    return out, S_new, hist_new


def run_trunk(x, c, pos, S_all, hist_all, chunked, w_ada, b_ada, g_ffn1, w_ffn1_in, w_ffn1_out,
              g_mix, w_in, gn_g, conv_w, conv_b, w_out, g_ffn2, w_ffn2_in, w_ffn2_out, g_final):
    B = x.shape[0]
    S_list, hist_list = [], []
    for l in range(DEPTH):
        mod = (jax.nn.silu(c) @ w_ada[l] + b_ada[l]).reshape(B, N_MOD, D_MODEL)
        sh1, sc1, gt1, shm, scm, gtm, sh2, sc2, gt2 = [mod[:, i] for i in range(N_MOD)]
        h = modulate(rmsnorm(x, g_ffn1[l]), sh1, sc1)
        x = x + 0.5 * gt1[:, None, :] * swiglu(h, w_ffn1_in[l], w_ffn1_out[l])
        h = modulate(rmsnorm(x, g_mix[l]), shm, scm)
        S_l = None if chunked else S_all[l]
        hist_l = jnp.zeros((B, CONV_WIDTH - 1, CONV_DIM), x.dtype) if chunked else hist_all[l]
        m, S_new, hist_new = mixer(h, pos, S_l, hist_l, chunked, w_in[l], w_out[l], gn_g[l], conv_w[l], conv_b[l])
        x = x + gtm[:, None, :] * m
        h = modulate(rmsnorm(x, g_ffn2[l]), sh2, sc2)
        x = x + 0.5 * gt2[:, None, :] * swiglu(h, w_ffn2_in[l], w_ffn2_out[l])
        S_list.append(S_new.astype(x.dtype))
        hist_list.append(hist_new.astype(x.dtype))
    return rmsnorm(x, g_final), jnp.stack(S_list), jnp.stack(hist_list)


def setup_inputs(seed: int = 0) -> dict:
    key = jax.random.key(seed)
    ks = jax.random.split(key, 24)
    f32 = jnp.float32
    nrm = lambda k, shape, s: jax.random.normal(k, shape, f32) * s
    return {
        "x_prompt": nrm(ks[0], (BATCH, SEQ, D_MODEL), 1.0),
        "x_sample": nrm(ks[1], (DEC_BATCH, DEC_SEQ, D_MODEL), 1.0),
        "c_prompt": nrm(ks[2], (BATCH, D_MODEL), 1.0),
        "c_sample": nrm(ks[3], (DEC_BATCH, D_MODEL), 1.0),
        "state_ret": nrm(ks[4], (DEPTH, DEC_BATCH, N_RET_HEADS, RET_HEAD_DIM, RET_HEAD_DIM), 0.5),
        "state_conv": nrm(ks[5], (DEPTH, DEC_BATCH, CONV_WIDTH - 1, CONV_DIM), 1.0),
        "w_ada": nrm(ks[6], (DEPTH, D_MODEL, N_MOD * D_MODEL), 0.5 * D_MODEL ** -0.5),
        "b_ada": nrm(ks[7], (DEPTH, N_MOD * D_MODEL), 0.02),
        "g_ffn1": 1.0 + nrm(ks[8], (DEPTH, D_MODEL), 0.1),
        "w_ffn1_in": nrm(ks[9], (DEPTH, D_MODEL, 2 * D_FF), D_MODEL ** -0.5),
        "w_ffn1_out": nrm(ks[10], (DEPTH, D_FF, D_MODEL), D_FF ** -0.5),
        "g_mix": 1.0 + nrm(ks[11], (DEPTH, D_MODEL), 0.1),
        "w_in": nrm(ks[12], (DEPTH, D_MODEL, IN_DIM), D_MODEL ** -0.5),
        "gn_g": 1.0 + nrm(ks[13], (DEPTH, RET_DIM), 0.1),
        "conv_w": nrm(ks[14], (DEPTH, CONV_WIDTH, CONV_DIM), CONV_WIDTH ** -0.5),
        "conv_b": nrm(ks[15], (DEPTH, CONV_DIM), 0.01),
        "w_out": nrm(ks[16], (DEPTH, MIX_DIM, D_MODEL), MIX_DIM ** -0.5),
        "g_ffn2": 1.0 + nrm(ks[17], (DEPTH, D_MODEL), 0.1),
        "w_ffn2_in": nrm(ks[18], (DEPTH, D_MODEL, 2 * D_FF), D_MODEL ** -0.5),
        "w_ffn2_out": nrm(ks[19], (DEPTH, D_FF, D_MODEL), D_FF ** -0.5),
        "g_final": 1.0 + nrm(ks[20], (D_MODEL,), 0.1),
    }


def reference(x_prompt, x_sample, c_prompt, c_sample, state_ret, state_conv, w_ada, b_ada, g_ffn1,
              w_ffn1_in, w_ffn1_out, g_mix, w_in, gn_g, conv_w, conv_b, w_out, g_ffn2, w_ffn2_in,
              w_ffn2_out, g_final):
    pos_prompt = jnp.arange(x_prompt.shape[1])
    pos_sample = PAST_LEN + jnp.arange(x_sample.shape[1])
    y_prompt, state_ret_prompt, state_conv_prompt = run_trunk(
        x_prompt, c_prompt, pos_prompt, None, None, True, w_ada, b_ada, g_ffn1, w_ffn1_in, w_ffn1_out,
        g_mix, w_in, gn_g, conv_w, conv_b, w_out, g_ffn2, w_ffn2_in, w_ffn2_out, g_final)
    y_sample, state_ret_sample, state_conv_sample = run_trunk(
        x_sample, c_sample, pos_sample, state_ret, state_conv, False, w_ada, b_ada, g_ffn1, w_ffn1_in, w_ffn1_out,
        g_mix, w_in, gn_g, conv_w, conv_b, w_out, g_ffn2, w_ffn2_in, w_ffn2_out, g_final)
    return (y_prompt, y_sample, state_ret_prompt, state_conv_prompt, state_ret_sample, state_conv_sample)
```

```python
import functools
import math

import jax
import jax.numpy as jnp
from jax import lax
from jax.experimental import pallas as pl
from jax.experimental.pallas import tpu as pltpu

CHUNK = 64
N_RET_HEADS = 4
CONV_WIDTH = 3
N_MOD = 9
PAST_LEN = 4096
ROPE_BASE = 10000.0
EPS = 1e-6

V7X_LANES = 128
V7X_VMEM_BYTES = 64 * 1024 * 1024
VMEM_LIMIT_BYTES = V7X_VMEM_BYTES - 8 * 1024 * 1024

F32 = jnp.float32
BF16 = jnp.bfloat16


def _dot(a, b):
    return jnp.dot(a, b, preferred_element_type=F32)


def _silu(x):
    return x * (1.0 / (1.0 + jnp.exp(-x)))


def _norm_modulate(x, g, shift, scale):
    ms = jnp.mean(x * x, axis=-1, keepdims=True)
    y = x * lax.rsqrt(ms + EPS) * g
    return y * (1.0 + scale) + shift


def _rope_body(cos_ref, sin_ref, *, pos0, half):
    rows = cos_ref.shape[0]
    shape = (rows, 2 * half)
    pos = lax.broadcasted_iota(jnp.int32, shape, 0) + (pl.program_id(0) * rows + pos0)
    lane = lax.broadcasted_iota(jnp.int32, shape, 1)
    upper = lane >= half
    j = jnp.where(upper, lane - half, lane).astype(F32)
    inv = jnp.exp(j * (-math.log(ROPE_BASE) / half))
    ang = pos.astype(F32) * inv
    sin = jnp.sin(ang)
    cos_ref[...] = jnp.cos(ang)
    sin_ref[...] = jnp.where(upper, sin, -sin)


def _rope_tables(n_pos, pos0, head_dim):
    rows = min(n_pos, 512)
    assert n_pos % rows == 0
    spec = pl.BlockSpec((rows, head_dim), lambda i: (i, 0))
    return pl.pallas_call(
        functools.partial(_rope_body, pos0=pos0, half=head_dim // 2),
        grid=(n_pos // rows,),
        out_specs=[spec, spec],
        out_shape=[jax.ShapeDtypeStruct((n_pos, head_dim), F32)] * 2,
        name="rope_tables",
    )()


def _decay_body(dmat_ref, cdec_ref, kdec_ref, gt_ref, *, chunk_shift):
    t = dmat_ref.shape[0]
    lanes = cdec_ref.shape[1]

    def log_gamma(shape):
        h = jnp.full(shape, pl.program_id(0), jnp.int32).astype(F32)
        return jnp.log1p(-jnp.exp2(-5.0 - h))

    i = lax.broadcasted_iota(jnp.int32, (t, t), 0)
    j = lax.broadcasted_iota(jnp.int32, (t, t), 1)
    dist = jnp.abs(i - j).astype(F32)
    visible = lax.shift_right_logical(j, chunk_shift) <= lax.shift_right_logical(i, chunk_shift)
    dmat_ref[...] = jnp.where(visible, jnp.exp(log_gamma((t, t)) * dist), 0.0)
    r = lax.broadcasted_iota(jnp.int32, (t, lanes), 0).astype(F32)
    lg = log_gamma((t, lanes))
    cdec_ref[...] = jnp.exp(lg * (r + 1.0))
    kdec_ref[...] = jnp.exp(lg * (t - 1.0 - r))
    gt_ref[...] = jnp.exp(log_gamma(gt_ref.shape) * float(t))


def _decay_tables(t, chunk, head_dim):
    assert t % chunk == 0 and chunk & (chunk - 1) == 0
    shift = chunk.bit_length() - 1
    return pl.pallas_call(
        functools.partial(_decay_body, chunk_shift=shift),
        grid=(N_RET_HEADS,),
        out_specs=[
            pl.BlockSpec((None, t, t), lambda h: (h, 0, 0)),
            pl.BlockSpec((None, t, head_dim), lambda h: (h, 0, 0)),
            pl.BlockSpec((None, t, head_dim), lambda h: (h, 0, 0)),
            pl.BlockSpec((None, 8, head_dim), lambda h: (h, 0, 0)),
        ],
        out_shape=[
            jax.ShapeDtypeStruct((N_RET_HEADS, t, t), F32),
            jax.ShapeDtypeStruct((N_RET_HEADS, t, head_dim), F32),
            jax.ShapeDtypeStruct((N_RET_HEADS, t, head_dim), F32),
            jax.ShapeDtypeStruct((N_RET_HEADS, 8, head_dim), F32),
        ],
        name="decay_tables",
    )()


def _ada_body(c_ref, w_ref, b_ref, o_ref):
    a = _silu(c_ref[...]).astype(BF16)
    o_ref[...] = _dot(a, w_ref[...].astype(BF16)) + b_ref[...]


def _ada_call(c_all, w_ada, b_ada):
    depth, d, n = w_ada.shape
    rows = c_all.shape[0]
    tn = n // 8
    assert n % tn == 0 and tn % V7X_LANES == 0
    return pl.pallas_call(
        _ada_body,
        grid=(depth, n // tn),
        in_specs=[
            pl.BlockSpec((rows, d), lambda l, j: (0, 0)),
            pl.BlockSpec((None, d, tn), lambda l, j: (l, 0, j)),
            pl.BlockSpec((None, 1, tn), lambda l, j: (l, 0, j)),
        ],
        out_specs=pl.BlockSpec((None, rows, tn), lambda l, j: (l, 0, j)),
        out_shape=jax.ShapeDtypeStruct((depth, rows, n), F32),
        compiler_params=pltpu.CompilerParams(vmem_limit_bytes=VMEM_LIMIT_BYTES),
        name="ada_mod",
    )(c_all, w_ada, b_ada.reshape(depth, 1, n))


def _mod_spec(layer, row0, bb, which, d):
    return pl.BlockSpec(
        (None, bb, None, 1, d), lambda b, t: (layer, row0 // bb + b, which, 0, 0)
    )


def _ffn_body(x_ref, sh_ref, sc_ref, gt_ref, g_ref, win_ref, wout_ref, *rest, f_chunk, final_norm):
    if final_norm:
        gfin_ref, o_ref = rest
    else:
        (o_ref,) = rest
    bb, tl, d = x_ref.shape
    d_ff = wout_ref.shape[0]
    x = x_ref[...]
    h = _norm_modulate(x, g_ref[...], sh_ref[...], sc_ref[...])
    hb = h.reshape(bb * tl, d).astype(BF16)
    acc = None
    for c in range(d_ff // f_chunk):
        lo = c * f_chunk
        gate = _dot(hb, win_ref[:, lo:lo + f_chunk])
        up = _dot(hb, win_ref[:, d_ff + lo:d_ff + lo + f_chunk])
        a = (_silu(gate) * up).astype(BF16)
        part = _dot(a, wout_ref[lo:lo + f_chunk, :])
        acc = part if acc is None else acc + part
    y = x + (0.5 * gt_ref[...]) * acc.reshape(bb, tl, d)
    if final_norm:
        ms = jnp.mean(y * y, axis=-1, keepdims=True)
        y = y * lax.rsqrt(ms + EPS) * gfin_ref[...]
    o_ref[...] = y


def _ffn_call(x, mod, layer, row0, which0, g_norm, w_in, w_out, g_final, *, bb, tl):
    b, l, d = x.shape
    d_ff = w_out.shape[1]
    f_chunk = d_ff // 2
    assert b % bb == 0 and l % tl == 0 and f_chunk % V7X_LANES == 0
    final_norm = g_final is not None
    resident = dict(pipeline_mode=pl.Buffered(1))
    in_specs = [
        pl.BlockSpec((bb, tl, d), lambda i, t: (i, t, 0)),
        _mod_spec(layer, row0, bb, which0, d),
        _mod_spec(layer, row0, bb, which0 + 1, d),
        _mod_spec(layer, row0, bb, which0 + 2, d),
        pl.BlockSpec((None, 1, d), lambda i, t: (layer, 0, 0)),
        pl.BlockSpec((None, d, 2 * d_ff), lambda i, t: (layer, 0, 0), **resident),
        pl.BlockSpec((None, d_ff, d), lambda i, t: (layer, 0, 0), **resident),
    ]
    args = [x, mod, mod, mod, g_norm, w_in, w_out]
    if final_norm:
        in_specs.append(pl.BlockSpec((1, d), lambda i, t: (0, 0)))
        args.append(g_final)
    return pl.pallas_call(
        functools.partial(_ffn_body, f_chunk=f_chunk, final_norm=final_norm),
        grid=(b // bb, l // tl),
        in_specs=in_specs,
        out_specs=pl.BlockSpec((bb, tl, d), lambda i, t: (i, t, 0)),
        out_shape=jax.ShapeDtypeStruct(x.shape, F32),
        compiler_params=pltpu.CompilerParams(
            dimension_semantics=("arbitrary", "arbitrary"), vmem_limit_bytes=VMEM_LIMIT_BYTES
        ),
        name="ffn",
    )(*args)


def _mixer_body(
    x_ref, sh_ref, sc_ref, gt_ref, g_ref, win_ref, wout_ref, gn_ref, cw_ref, cb_ref,
    cos_ref, sin_ref, dmat_ref, cdec_ref, kdec_ref, gpow_ref, s0_ref, h0_ref,
    o_ref, s_ref, hist_ref, mix_ref,
):
    _, tl, d = x_ref.shape
    n_heads, t_blk, _ = dmat_ref.shape
    hd = cos_ref.shape[1]
    ret = n_heads * hd
    conv = cw_ref.shape[1]

    @pl.when(pl.program_id(1) == 0)
    def _():
        s_ref[...] = s0_ref[...]
        hist_ref[...] = h0_ref[...]

    x = x_ref[0]
    h = _norm_modulate(x, g_ref[...], sh_ref[0], sc_ref[0])
    p = _dot(h.astype(BF16), win_ref[...])

    cos = cos_ref[...]
    sin = sin_ref[...]
    k_scale = hd ** -0.5
    for head in range(n_heads):
        lo = head * hd
        q = p[:, lo:lo + hd]
        k = p[:, ret + lo:ret + lo + hd]
        q = q * cos + pltpu.roll(q, hd // 2, 1) * sin
        k = (k * cos + pltpu.roll(k, hd // 2, 1) * sin) * k_scale
        v = p[:, 2 * ret + lo:2 * ret + lo + hd]
        gate = p[:, 3 * ret + lo:3 * ret + lo + hd]
        gn_g = gn_ref[:, lo:lo + hd]
        for blk in range(tl // t_blk):
            r0 = blk * t_blk
            qb = q[r0:r0 + t_blk].astype(BF16)
            kf = k[r0:r0 + t_blk]
            kb = kf.astype(BF16)
            vb = v[r0:r0 + t_blk].astype(BF16)
            state = s_ref[0, head]
            scores = lax.dot_general(qb, kb, (((1,), (1,)), ((), ())), preferred_element_type=F32)
            scores = scores * dmat_ref[head]
            o = _dot(scores.astype(BF16), vb) + _dot(qb, state.astype(BF16)) * cdec_ref[head]
            kd = (kf * kdec_ref[head]).astype(BF16)
            s_ref[0, head] = gpow_ref[head, 0:1, :] * state + lax.dot_general(
                kd, vb, (((0,), (0,)), ((), ())), preferred_element_type=F32
            )
            mu = jnp.mean(o, axis=-1, keepdims=True)
            oc = o - mu
            var = jnp.mean(oc * oc, axis=-1, keepdims=True)
            o_n = oc * lax.rsqrt(var + EPS) * gn_g
            mix_ref[r0:r0 + t_blk, lo:lo + hd] = (_silu(gate[r0:r0 + t_blk]) * o_n).astype(BF16)

    bg = p[:, 4 * ret:4 * ret + conv]
    u = p[:, 4 * ret + conv:4 * ret + 2 * conv] * p[:, 4 * ret + 2 * conv:4 * ret + 3 * conv]
    hist = hist_ref[0]
    row = lax.broadcasted_iota(jnp.int32, u.shape, 0)
    u1 = jnp.where(row == 0, hist[1:2], pltpu.roll(u, 1, 0))
    u2 = jnp.where(row == 0, hist[0:1], jnp.where(row == 1, hist[1:2], pltpu.roll(u, 2, 0)))
    y = cb_ref[...] + u2 * cw_ref[0:1, :] + u1 * cw_ref[1:2, :] + u * cw_ref[2:3, :]
    mix_ref[:, ret:ret + conv] = (bg * y).astype(BF16)
    hist_ref[0] = pltpu.roll(u[tl - 8:tl], 2, 0)[0:2]

    out = _dot(mix_ref[...], wout_ref[...])
    o_ref[0] = x + gt_ref[0] * out


def _mixer_call(x, mod, layer, row0, g_mix, w_in, w_out, gn_g, conv_w, conv_b,
                cos, sin, tables, s0, h0, *, tl):
    b, l, d = x.shape
    in_dim = w_in.shape[2]
    mix_dim = w_out.shape[1]
    dmat, cdec, kdec, gpow = tables
    n_heads, t_blk, _ = dmat.shape
    hd = cos.shape[1]
    conv = conv_w.shape[2]
    assert l % tl == 0 and tl % t_blk == 0 and tl % 16 == 0 and tl >= 8
    resident = dict(pipeline_mode=pl.Buffered(1))

    def full(shape):
        return pl.BlockSpec(shape, lambda i, t: (0,) * len(shape), **resident)

    in_specs = [
        pl.BlockSpec((1, tl, d), lambda i, t: (i, t, 0)),
        _mod_spec(layer, row0, 1, 3, d),
        _mod_spec(layer, row0, 1, 4, d),
        _mod_spec(layer, row0, 1, 5, d),
        pl.BlockSpec((None, 1, d), lambda i, t: (layer, 0, 0)),
        pl.BlockSpec((None, d, in_dim), lambda i, t: (layer, 0, 0), **resident),
        pl.BlockSpec((None, mix_dim, d), lambda i, t: (layer, 0, 0), **resident),
        pl.BlockSpec((None, 1, n_heads * hd), lambda i, t: (layer, 0, 0)),
        pl.BlockSpec((None, CONV_WIDTH, conv), lambda i, t: (layer, 0, 0)),
        pl.BlockSpec((None, 1, conv), lambda i, t: (layer, 0, 0)),
        pl.BlockSpec((tl, hd), lambda i, t: (t, 0)),
        pl.BlockSpec((tl, hd), lambda i, t: (t, 0)),
        full(dmat.shape),
        full(cdec.shape),
        full(kdec.shape),
        full(gpow.shape),
        pl.BlockSpec((1, n_heads, hd, hd), lambda i, t: (i, 0, 0, 0)),
        pl.BlockSpec((1, CONV_WIDTH - 1, conv), lambda i, t: (i, 0, 0)),
    ]
    out_specs = [
        pl.BlockSpec((1, tl, d), lambda i, t: (i, t, 0)),
        pl.BlockSpec((1, n_heads, hd, hd), lambda i, t: (i, 0, 0, 0)),
        pl.BlockSpec((1, CONV_WIDTH - 1, conv), lambda i, t: (i, 0, 0)),
    ]
    out_shape = [
        jax.ShapeDtypeStruct(x.shape, F32),
        jax.ShapeDtypeStruct((b, n_heads, hd, hd), F32),
        jax.ShapeDtypeStruct((b, CONV_WIDTH - 1, conv), F32),
    ]
    return pl.pallas_call(
        _mixer_body,
        grid=(b, l // tl),
        in_specs=in_specs,
        out_specs=out_specs,
        out_shape=out_shape,
        scratch_shapes=[pltpu.VMEM((tl, mix_dim), BF16)],
        compiler_params=pltpu.CompilerParams(
            dimension_semantics=("arbitrary", "arbitrary"), vmem_limit_bytes=VMEM_LIMIT_BYTES
        ),
        name="mixer",
    )(x, mod, mod, mod, g_mix, w_in, w_out, gn_g, conv_w, conv_b,
      cos, sin, dmat, cdec, kdec, gpow, s0, h0)


def _run_trunk(x, mod, row0, pos0, s_all, hist_all, chunk, weights, *, ffn_tile, mixer_tile, t_blk):
    (g_ffn1, w_ffn1_in, w_ffn1_out, g_mix, w_in, gn_g, conv_w, conv_b, w_out,
     g_ffn2, w_ffn2_in, w_ffn2_out, g_final) = weights
    b, l, d = x.shape
    depth = w_in.shape[0]
    hd = gn_g.shape[-1] // N_RET_HEADS
    cos, sin = _rope_tables(l, pos0, hd)
    tables = _decay_tables(t_blk, chunk, hd)
    bb, tl = ffn_tile
    s_list, hist_list = [], []
    for layer in range(depth):
        x = _ffn_call(x, mod, layer, row0, 0, g_ffn1, w_ffn1_in, w_ffn1_out, None, bb=bb, tl=tl)
        x, s_new, hist_new = _mixer_call(
            x, mod, layer, row0, g_mix, w_in, w_out, gn_g, conv_w, conv_b,
            cos, sin, tables, s_all[layer], hist_all[layer], tl=mixer_tile)
        last = layer == depth - 1
        x = _ffn_call(x, mod, layer, row0, 6, g_ffn2, w_ffn2_in, w_ffn2_out,
                      g_final if last else None, bb=bb, tl=tl)
        s_list.append(s_new)
        hist_list.append(hist_new)
    return x, jnp.stack(s_list), jnp.stack(hist_list)


def kernel(x_prompt, x_sample, c_prompt, c_sample, state_ret, state_conv, w_ada, b_ada, g_ffn1,
           w_ffn1_in, w_ffn1_out, g_mix, w_in, gn_g, conv_w, conv_b, w_out, g_ffn2, w_ffn2_in,
           w_ffn2_out, g_final):
    depth, d, _ = w_ada.shape
    batch, seq, _ = x_prompt.shape
    dec_batch, dec_seq, _ = x_sample.shape
    n_heads, hd = state_ret.shape[2], state_ret.shape[3]
    conv = state_conv.shape[3]
    assert n_heads == N_RET_HEADS

    c_all = jnp.concatenate([c_prompt, c_sample], axis=0)
    mod = _ada_call(c_all, w_ada, b_ada).reshape(depth, batch + dec_batch, N_MOD, 1, d)

    def vec(g):
        return g.reshape(depth, 1, g.shape[-1])

    weights = (
        vec(g_ffn1), w_ffn1_in.astype(BF16), w_ffn1_out.astype(BF16),
        vec(g_mix), w_in.astype(BF16), vec(gn_g), conv_w, vec(conv_b), w_out.astype(BF16),
        vec(g_ffn2), w_ffn2_in.astype(BF16), w_ffn2_out.astype(BF16), g_final.reshape(1, d),
    )

    zero_s = jnp.zeros((depth, batch, n_heads, hd, hd), F32)
    zero_h = jnp.zeros((depth, batch, CONV_WIDTH - 1, conv), F32)
    y_prompt, s_prompt, h_prompt = _run_trunk(
        x_prompt, mod, 0, 0, zero_s, zero_h, CHUNK, weights,
        ffn_tile=(1, 512), mixer_tile=512, t_blk=256)
    y_sample, s_sample, h_sample = _run_trunk(
        x_sample, mod, batch, PAST_LEN, state_ret, state_conv, dec_seq, weights,
        ffn_tile=(dec_batch, dec_seq), mixer_tile=dec_seq, t_blk=dec_seq)
    return (y_prompt, y_sample, s_prompt, h_prompt, s_sample, h_sample)
```
